```python
import math
import jax
import jax.numpy as jnp
from jax import lax
import numpy as np

D_MODEL = 1024
BATCH = 16
SEQ = 2048
DEPTH = 2

HEAD_DIM = 64
BLOCK = 128
GMLP_GROUPS = 8
GMLP_CHUNK = 128
GMLP_WIDTH = GMLP_GROUPS * HEAD_DIM
SWA_Q_HEADS = 8
SWA_KV_HEADS = 2
SWA_WINDOW = 128
FOX_HEADS = 8
CONV_CH = 512
CONV_TAPS = 31
N_BRANCH = 4
BRANCH_WIDTH = 512
ROPE_THETA = 10000.0
D_FF = 3584
N_EXPERTS = 8
TOP_K = 2
N_DENSE = (DEPTH + 1) // 2
N_MOE = DEPTH // 2
EPS = 1e-6
NEG_INF = -1e30
IN_SPLIT = (GMLP_WIDTH, GMLP_WIDTH,
            SWA_Q_HEADS * HEAD_DIM, SWA_KV_HEADS * HEAD_DIM, SWA_KV_HEADS * HEAD_DIM,
            FOX_HEADS * HEAD_DIM, FOX_HEADS * HEAD_DIM, FOX_HEADS * HEAD_DIM, FOX_HEADS,
            CONV_CH, CONV_CH, N_BRANCH * D_MODEL)
N_IN = sum(IN_SPLIT)

kernel_name = 'hybrid_gated_gmlp_swa_fox_conv_moe'


def _rms_norm(x, g):
    xf = x.astype(jnp.float32)
    y = xf * lax.rsqrt(jnp.mean(xf * xf, axis=-1, keepdims=True) + EPS)
    return (y * g.astype(jnp.float32)).astype(x.dtype)


def _layer_norm(x, g, b):
    xf = x.astype(jnp.float32)
    xc = xf - jnp.mean(xf, axis=-1, keepdims=True)
    y = xc * lax.rsqrt(jnp.mean(xc * xc, axis=-1, keepdims=True) + EPS)
    return (y * g.astype(jnp.float32) + b.astype(jnp.float32)).astype(x.dtype)


def _rope_tables(positions):
    inv_freq = jnp.exp(-math.log(ROPE_THETA) * jnp.arange(0, HEAD_DIM, 2, dtype=jnp.float32) / HEAD_DIM)
    ang = positions.astype(jnp.float32)[..., None] * inv_freq
    return jnp.cos(ang)[:, :, None, :], jnp.sin(ang)[:, :, None, :]


def _apply_rope(t, cos, sin):
    tf = t.astype(jnp.float32)
    t1, t2 = tf[..., :HEAD_DIM // 2], tf[..., HEAD_DIM // 2:]
    return jnp.concatenate([t1 * cos - t2 * sin, t2 * cos + t1 * sin], axis=-1).astype(t.dtype)


def _to_blocks(t):
    b, s = t.shape[0], t.shape[1]
    t = t.reshape((b, s // BLOCK, BLOCK) + t.shape[2:])
    return jnp.moveaxis(t, 1, 0)


def _from_blocks(t):
    t = jnp.moveaxis(t, 0, 1)
    return t.reshape((t.shape[0], t.shape[1] * t.shape[2]) + t.shape[3:])


def _gmlp_mixer(zu, zv, ln_g, ln_b, ws, bs):
    b, s, _ = zu.shape
    u = jax.nn.gelu(zu)
    v = _layer_norm(jax.nn.gelu(zv), ln_g, ln_b)
    v = v.reshape(b, s // GMLP_CHUNK, GMLP_CHUNK, GMLP_GROUPS, HEAD_DIM)
    causal = jnp.tril(jnp.ones((GMLP_CHUNK, GMLP_CHUNK), dtype=bool))
    w_s = jnp.where(causal, ws, 0.0).astype(v.dtype)
    mixed = jnp.einsum('gts,bnsgc->bntgc', w_s, v) + bs.T.astype(v.dtype)[None, None, :, :, None]
    return u * mixed.reshape(b, s, GMLP_WIDTH)


def _swa_mixer(q, k, v, sinks):
    b, s = q.shape[0], q.shape[1]
    nb = s // BLOCK
    groups = SWA_Q_HEADS // SWA_KV_HEADS
    qb = _to_blocks(q.reshape(b, s, SWA_KV_HEADS, groups, HEAD_DIM))
    kb, vb = _to_blocks(k), _to_blocks(v)
    shift = lambda t: jnp.concatenate([jnp.zeros_like(t[:1]), t[:-1]], axis=0)
    kk = jnp.concatenate([shift(kb), kb], axis=2)
    vv = jnp.concatenate([shift(vb), vb], axis=2)
    qi = jnp.arange(BLOCK)[:, None]
    kj = jnp.arange(2 * BLOCK)[None, :]
    band = (kj > qi + BLOCK - SWA_WINDOW) & (kj <= qi + BLOCK)
    sink = sinks.astype(jnp.float32).reshape(1, SWA_KV_HEADS, groups, 1, 1)
    scale = HEAD_DIM ** -0.5

    def block_attn(args):
        qblk, kblk, vblk, idx = args
        sc = jnp.einsum('bqhgd,bkhd->bhgqk', qblk, kblk, preferred_element_type=jnp.float32) * scale
        valid = band & (idx * BLOCK - BLOCK + kj >= 0)
        sc = jnp.where(valid, sc, NEG_INF)
        m = jnp.maximum(jnp.max(sc, axis=-1, keepdims=True), sink)
        p = jnp.exp(sc - m)
        p = p / (jnp.sum(p, axis=-1, keepdims=True) + jnp.exp(sink - m))
        return jnp.einsum('bhgqk,bkhd->bqhgd', p.astype(vblk.dtype), vblk)

    out = lax.map(block_attn, (qb, kk, vv, jnp.arange(nb)))
    return _from_blocks(out).reshape(b, s, SWA_Q_HEADS * HEAD_DIM)


def _fox_mixer(q, k, v, f_logit):
    b, s = q.shape[0], q.shape[1]
    nb = s // BLOCK
    cum = jnp.cumsum(jax.nn.log_sigmoid(f_logit.astype(jnp.float32)), axis=1)
    cum_k = jnp.transpose(cum, (0, 2, 1))[:, :, None, :]
    kpos = jnp.arange(s)
    scale = HEAD_DIM ** -0.5

    def block_attn(args):
        qblk, cq, idx = args
        sc = jnp.einsum('bqhd,bkhd->bhqk', qblk, k, preferred_element_type=jnp.float32) * scale
        sc = sc + jnp.transpose(cq, (0, 2, 1))[..., None] - cum_k
        qpos = idx * BLOCK + jnp.arange(BLOCK)
        sc = jnp.where(kpos[None, :] <= qpos[:, None], sc, NEG_INF)
        p = jax.nn.softmax(sc, axis=-1)
        return jnp.einsum('bhqk,bkhd->bqhd', p.astype(v.dtype), v)

    out = lax.map(block_attn, (_to_blocks(q), _to_blocks(cum), jnp.arange(nb)))
    return _from_blocks(out).reshape(b, s, FOX_HEADS * HEAD_DIM)


def _conv_mixer(za, zg, w, bias, ln_g, ln_b):
    y = za * jax.nn.sigmoid(zg)
    y = lax.conv_general_dilated(y, w[:, None, :].astype(y.dtype), window_strides=(1,),
                                 padding=((CONV_TAPS - 1, 0),),
                                 dimension_numbers=('NWC', 'WIO', 'NWC'),
                                 feature_group_count=CONV_CH)
    y = y + bias.astype(y.dtype)
    return jax.nn.silu(_layer_norm(y, ln_g, ln_b))


def _mixer_layer(h, cos, sin, w_in, gmlp_ln_g, gmlp_ln_b, gmlp_ws, gmlp_bs, swa_sink, fox_bf,
                 conv_w, conv_b, conv_ln_g, conv_ln_b, w_branch, w_out):
    b, s, _ = h.shape
    z = jnp.einsum('bsd,dn->bsn', h, w_in)
    parts, off = [], 0
    for n in IN_SPLIT:
        parts.append(z[..., off:off + n])
        off += n
    zu, zv, sq, sk, sv, fq, fk, fv, ff, ca, cg, zgate = parts
    heads = lambda t, n: t.reshape(b, s, n, HEAD_DIM)
    o_a = _gmlp_mixer(zu, zv, gmlp_ln_g, gmlp_ln_b, gmlp_ws, gmlp_bs)
    o_b = _swa_mixer(_apply_rope(heads(sq, SWA_Q_HEADS), cos, sin),
                     _apply_rope(heads(sk, SWA_KV_HEADS), cos, sin),
                     heads(sv, SWA_KV_HEADS), swa_sink)
    o_c = _fox_mixer(heads(fq, FOX_HEADS), heads(fk, FOX_HEADS), heads(fv, FOX_HEADS),
                     ff + fox_bf.astype(ff.dtype))
    o_d = _conv_mixer(ca, cg, conv_w, conv_b, conv_ln_g, conv_ln_b)
    branches = jnp.stack([o_a, o_b, o_c, o_d], axis=2)
    proj = jnp.einsum('bsnc,ncd->bsnd', branches, w_branch)
    gates = jax.nn.sigmoid(zgate.reshape(b, s, N_BRANCH, D_MODEL))
    merged = jnp.sum(gates * proj, axis=2)
    return jnp.einsum('bsd,de->bse', merged, w_out)


def _swiglu(h, wg, wu, wd):
    return (jax.nn.silu(h @ wg) * (h @ wu)) @ wd


def _moe(h, router_w, wg, wu, wd):
    b, s, d = h.shape
    t = h.reshape(b * s, d)
    logits = jnp.einsum('td,de->te', t, router_w, preferred_element_type=jnp.float32)
    vals, idx = lax.top_k(logits, TOP_K)
    w = jax.nn.softmax(vals, axis=-1)
    gate = jnp.sum(jax.nn.one_hot(idx, N_EXPERTS, dtype=jnp.float32) * w[..., None], axis=1)
    y = jnp.zeros_like(t)
    for e in range(N_EXPERTS):
        y = y + gate[:, e:e + 1].astype(t.dtype) * _swiglu(t, wg[e], wu[e], wd[e])
    return y.reshape(b, s, d)


def setup_inputs(seed: int = 0) -> dict:
    key = jax.random.key(seed)
    ks = jax.random.split(key, 26)
    f32 = jnp.float32
    nrm = lambda k, shape, sc: jax.random.normal(k, shape, f32) * sc
    offset = jax.random.randint(ks[1], (BATCH, 1), 0, 4096, dtype=jnp.int32)
    positions = (offset + jnp.arange(SEQ, dtype=jnp.int32)[None, :]).astype(jnp.int32)
    return {
        'x': nrm(ks[0], (BATCH, SEQ, D_MODEL), 1.0),
        'positions': positions,
        'norm_mix_g': 1.0 + nrm(ks[2], (DEPTH, D_MODEL), 0.05),
        'w_in': nrm(ks[3], (DEPTH, D_MODEL, N_IN), D_MODEL ** -0.5),
        'gmlp_ln_g': 1.0 + nrm(ks[4], (DEPTH, GMLP_WIDTH), 0.05),
        'gmlp_ln_b': nrm(ks[5], (DEPTH, GMLP_WIDTH), 0.05),
        'gmlp_ws': nrm(ks[6], (DEPTH, GMLP_GROUPS, GMLP_CHUNK, GMLP_CHUNK), GMLP_CHUNK ** -0.5),
        'gmlp_bs': 1.0 + nrm(ks[7], (DEPTH, GMLP_GROUPS, GMLP_CHUNK), 0.05),
        'swa_sink': nrm(ks[8], (DEPTH, SWA_Q_HEADS), 0.5),
        'fox_bf': 2.0 + nrm(ks[9], (DEPTH, FOX_HEADS), 0.5),
        'conv_w': nrm(ks[10], (DEPTH, CONV_TAPS, CONV_CH), CONV_TAPS ** -0.5),
        'conv_b': nrm(ks[11], (DEPTH, CONV_CH), 0.02),
        'conv_ln_g': 1.0 + nrm(ks[12], (DEPTH, CONV_CH), 0.05),
        'conv_ln_b': nrm(ks[13], (DEPTH, CONV_CH), 0.05),
        'w_branch': nrm(ks[14], (DEPTH, N_BRANCH, BRANCH_WIDTH, D_MODEL), BRANCH_WIDTH ** -0.5),
        'w_out': nrm(ks[15], (DEPTH, D_MODEL, D_MODEL), D_MODEL ** -0.5),
        'norm_ffn_g': 1.0 + nrm(ks[16], (DEPTH, D_MODEL), 0.05),
        'ffn_w_gate': nrm(ks[17], (N_DENSE, D_MODEL, D_FF), D_MODEL ** -0.5),
        'ffn_w_up': nrm(ks[18], (N_DENSE, D_MODEL, D_FF), D_MODEL ** -0.5),
        'ffn_w_down': nrm(ks[19], (N_DENSE, D_FF, D_MODEL), D_FF ** -0.5),
        'router_w': nrm(ks[20], (N_MOE, D_MODEL, N_EXPERTS), D_MODEL ** -0.5),
        'exp_w_gate': nrm(ks[21], (N_MOE, N_EXPERTS, D_MODEL, D_FF), D_MODEL ** -0.5),
        'exp_w_up': nrm(ks[22], (N_MOE, N_EXPERTS, D_MODEL, D_FF), D_MODEL ** -0.5),
        'exp_w_down': nrm(ks[23], (N_MOE, N_EXPERTS, D_FF, D_MODEL), D_FF ** -0.5),
        'norm_final_g': 1.0 + nrm(ks[24], (D_MODEL,), 0.05),
    }


def reference(x, positions, norm_mix_g, w_in, gmlp_ln_g, gmlp_ln_b, gmlp_ws, gmlp_bs, swa_sink, fox_bf,
              conv_w, conv_b, conv_ln_g, conv_ln_b, w_branch, w_out, norm_ffn_g,
              ffn_w_gate, ffn_w_up, ffn_w_down, router_w, exp_w_gate, exp_w_up, exp_w_down,
              norm_final_g):
    cos, sin = _rope_tables(positions)
    for l in range(DEPTH):
        h = _rms_norm(x, norm_mix_g[l])
        x = x + _mixer_layer(h, cos, sin, w_in[l], gmlp_ln_g[l], gmlp_ln_b[l], gmlp_ws[l], gmlp_bs[l],
                             swa_sink[l], fox_bf[l], conv_w[l], conv_b[l], conv_ln_g[l], conv_ln_b[l],
                             w_branch[l], w_out[l])
        h = _rms_norm(x, norm_ffn_g[l])
        j = l // 2
        if l % 2 == 0:
            x = x + _swiglu(h, ffn_w_gate[j], ffn_w_up[j], ffn_w_down[j])
        else:
            x = x + _moe(h, router_w[j], exp_w_gate[j], exp_w_up[j], exp_w_down[j])
    return _rms_norm(x, norm_final_g)
```

```python
import functools
import math

import jax
import jax.numpy as jnp
from jax import lax
from jax.experimental import pallas as pl
from jax.experimental.pallas import tpu as pltpu

F32 = jnp.float32
BF16 = jnp.bfloat16
HIGHEST = lax.Precision.HIGHEST

D_MODEL = 1024
HEAD_DIM = 64
BLOCK = 128
LANES = 128
N_HEADS = 8
SWA_KV_HEADS = 2
N_PAIRS = N_HEADS // 2
MIX_W = 512
CONV_TAPS = 31
CONV_HALO = 32
N_BRANCH = 4
D_FF = 3584
N_EXPERTS = 8
ROPE_THETA = 10000.0
EPS = 1e-6
NEG_INF = -1e30
SCALE = HEAD_DIM ** -0.5

Z_GATE = 0
Z_ZU = 4096
Z_ZV = 4608
Z_SQ = 5120
Z_FQ = 5632
Z_FK = 6144
Z_FV = 6656
Z_CA = 7168
Z_CG = 7680
Z_SK = 8192
Z_SV = 8320
Z_COLS = 8448

PROJ_TM = 1024
PROJ_TN = 768
GMLP_T = 512
CONV_T = 256
CONV_RB = 32
MERGE_TM = 512
FFN_TM = 1024
FFN_TF = 512
ROUTE_TM = 512
MOE_TR = 512
GATHER_CHUNK = 1024
COMBINE_TM = 512
VMEM_LIMIT = 56 * 1024 * 1024


def _params(*sem):
    return pltpu.CompilerParams(dimension_semantics=sem, vmem_limit_bytes=VMEM_LIMIT)


def _rms(x, g):
    return x * lax.rsqrt(jnp.mean(x * x, axis=-1, keepdims=True) + EPS) * g


def _sigmoid(x):
    return 1.0 / (1.0 + jnp.exp(-x))


def _lane_lo(shape):
    lane = lax.broadcasted_iota(jnp.int32, shape, len(shape) - 1)
    return (lane & (LANES - 1)) < HEAD_DIM


def _rope_kernel(pos_ref, cos_ref, sin_ref):
    lane = lax.broadcasted_iota(jnp.int32, cos_ref.shape, 1)
    fidx = (lane & (HEAD_DIM // 2 - 1)).astype(F32)
    inv_freq = jnp.exp(fidx * (-math.log(ROPE_THETA) * 2.0 / HEAD_DIM))
    ang = pos_ref[...].astype(F32) * inv_freq
    first = (lane & (HEAD_DIM - 1)) < HEAD_DIM // 2
    cos_ref[...] = jnp.cos(ang)
    sin_ref[...] = jnp.where(first, -jnp.sin(ang), jnp.sin(ang))


def _rope_tables(pos):
    t = pos.shape[0]
    tm = 1024
    return pl.pallas_call(
        _rope_kernel,
        grid=(t // tm,),
        in_specs=[pl.BlockSpec((tm, 1), lambda i: (i, 0))],
        out_specs=[pl.BlockSpec((tm, LANES), lambda i: (i, 0))] * 2,
        out_shape=[jax.ShapeDtypeStruct((t, LANES), F32)] * 2,
        compiler_params=_params("arbitrary"),
        name="rope_tables",
    )(pos)


def _rope(x, cos, sin):
    lane = lax.broadcasted_iota(jnp.int32, x.shape, 1)
    first = (lane & (HEAD_DIM - 1)) < HEAD_DIM // 2
    partner = jnp.where(first, pltpu.roll(x, LANES - HEAD_DIM // 2, 1),
                        pltpu.roll(x, HEAD_DIM // 2, 1))
    return x * cos + partner * sin


def _proj_kernel(x_ref, g_ref, w_ref, wff_ref, z_ref, zff_ref, h_scr):
    @pl.when(pl.program_id(1) == 0)
    def _():
        h = _rms(x_ref[...], g_ref[...])
        h_scr[...] = h.astype(BF16)
        zff_ref[...] = jnp.dot(h, wff_ref[...], preferred_element_type=F32, precision=HIGHEST)

    z_ref[...] = jnp.dot(h_scr[...], w_ref[...], preferred_element_type=F32).astype(BF16)


def _proj(x, g, w, wff):
    t = x.shape[0]
    tm = min(PROJ_TM, t)
    return pl.pallas_call(
        _proj_kernel,
        grid=(t // tm, Z_COLS // PROJ_TN),
        in_specs=[
            pl.BlockSpec((tm, D_MODEL), lambda i, j: (i, 0)),
            pl.BlockSpec((1, D_MODEL), lambda i, j: (0, 0)),
            pl.BlockSpec((D_MODEL, PROJ_TN), lambda i, j: (0, j)),
            pl.BlockSpec((D_MODEL, LANES), lambda i, j: (0, 0)),
        ],
        out_specs=[
            pl.BlockSpec((tm, PROJ_TN), lambda i, j: (i, j)),
            pl.BlockSpec((tm, LANES), lambda i, j: (i, 0)),
        ],
        out_shape=[jax.ShapeDtypeStruct((t, Z_COLS), BF16),
                   jax.ShapeDtypeStruct((t, LANES), F32)],
        scratch_shapes=[pltpu.VMEM((tm, D_MODEL), BF16)],
        compiler_params=_params("arbitrary", "arbitrary"),
        name="proj",
    )(x, g, w, wff)


def _cum_kernel(zff_ref, bf_ref, o_ref):
    nb = zff_ref.shape[0] // BLOCK
    row = lax.broadcasted_iota(jnp.int32, (BLOCK, BLOCK), 0)
    col = lax.broadcasted_iota(jnp.int32, (BLOCK, BLOCK), 1)
    tri = (row >= col).astype(F32)
    carry = jnp.zeros((1, LANES), F32)
    for j in range(nb):
        x = zff_ref[j * BLOCK:(j + 1) * BLOCK, :] + bf_ref[...]
        ls = jnp.minimum(x, 0.0) - jnp.log(1.0 + jnp.exp(-jnp.abs(x)))
        c = jnp.dot(tri, ls, preferred_element_type=F32, precision=HIGHEST) + carry
        carry = c[BLOCK - 1:BLOCK, :]
        ct = c.T
        for p in range(N_PAIRS):
            o_ref[0, p, j] = ct[2 * p:2 * p + 2, :]


def _cum(zff, bf, b, s):
    nb = s // BLOCK
    return pl.pallas_call(
        _cum_kernel,
        grid=(b,),
        in_specs=[pl.BlockSpec((s, LANES), lambda i: (i, 0)),
                  pl.BlockSpec((1, LANES), lambda i: (0, 0))],
        out_specs=pl.BlockSpec((1, N_PAIRS, nb, 2, LANES), lambda i: (i, 0, 0, 0, 0)),
        out_shape=jax.ShapeDtypeStruct((b, N_PAIRS, nb, 2, LANES), F32),
        compiler_params=_params("arbitrary"),
        name="fox_cum",
    )(zff, bf)


def _gmlp_kernel(zu_ref, zv_ref, lng_ref, lnb_ref, ws_ref, bias_ref, o_ref):
    nc = zu_ref.shape[0] // BLOCK
    row = lax.broadcasted_iota(jnp.int32, (BLOCK, BLOCK), 0)
    col = lax.broadcasted_iota(jnp.int32, (BLOCK, BLOCK), 1)
    causal = row >= col
    v = jax.nn.gelu(zv_ref[...].astype(F32))
    mu = jnp.mean(v, axis=-1, keepdims=True)
    vc = v - mu
    vn = vc * lax.rsqrt(jnp.mean(vc * vc, axis=-1, keepdims=True) + EPS)
    vn = (vn * lng_ref[...] + lnb_ref[...]).astype(BF16)
    lo = _lane_lo((BLOCK, LANES))
    for p in range(N_PAIRS):
        rhs = jnp.concatenate(
            [vn[c * BLOCK:(c + 1) * BLOCK, p * LANES:(p + 1) * LANES] for c in range(nc)], axis=1)
        w0 = jnp.where(causal, ws_ref[2 * p], 0.0).astype(BF16)
        w1 = jnp.where(causal, ws_ref[2 * p + 1], 0.0).astype(BF16)
        m0 = jnp.dot(w0, rhs, preferred_element_type=F32)
        m1 = jnp.dot(w1, rhs, preferred_element_type=F32)
        for c in range(nc):
            mixed = jnp.where(lo, m0[:, c * LANES:(c + 1) * LANES], m1[:, c * LANES:(c + 1) * LANES])
            mixed = mixed + bias_ref[:, p * LANES:(p + 1) * LANES]
            u = jax.nn.gelu(zu_ref[c * BLOCK:(c + 1) * BLOCK, p * LANES:(p + 1) * LANES].astype(F32))
            o_ref[c * BLOCK:(c + 1) * BLOCK, p * LANES:(p + 1) * LANES] = (u * mixed).astype(BF16)


def _gmlp(z, lng, lnb, ws, bias):
    t = z.shape[0]
    tg = min(GMLP_T, t)
    return pl.pallas_call(
        _gmlp_kernel,
        grid=(t // tg,),
        in_specs=[
            pl.BlockSpec((tg, MIX_W), lambda i: (i, Z_ZU // MIX_W)),
            pl.BlockSpec((tg, MIX_W), lambda i: (i, Z_ZV // MIX_W)),
            pl.BlockSpec((1, MIX_W), lambda i: (0, 0)),
            pl.BlockSpec((1, MIX_W), lambda i: (0, 0)),
            pl.BlockSpec((N_HEADS, BLOCK, BLOCK), lambda i: (0, 0, 0)),
            pl.BlockSpec((BLOCK, MIX_W), lambda i: (0, 0)),
        ],
        out_specs=pl.BlockSpec((tg, MIX_W), lambda i: (i, 0)),
        out_shape=jax.ShapeDtypeStruct((t, MIX_W), BF16),
        compiler_params=_params("arbitrary"),
        name="gmlp",
    )(z, z, lng, lnb, ws, bias)


def _swa_kernel(sink_ref, q_ref, kp_ref, kc_ref, vp_ref, vc_ref,
                cosc_ref, sinc_ref, cosp_ref, sinp_ref, o_ref):
    i = pl.program_id(1)
    lo = _lane_lo((BLOCK, LANES))
    lo2 = _lane_lo((2 * BLOCK, LANES))
    cos_c, sin_c = cosc_ref[...], sinc_ref[...]
    kc = _rope(kc_ref[...].astype(F32), cos_c, sin_c)
    kp = _rope(kp_ref[...].astype(F32), cosp_ref[...], sinp_ref[...])
    k2 = jnp.concatenate([kp, kc], axis=0)
    v2 = jnp.concatenate([vp_ref[...], vc_ref[...]], axis=0).astype(F32)
    k2r = pltpu.roll(k2, HEAD_DIM, 1)
    v2r = pltpu.roll(v2, HEAD_DIM, 1)
    kd = [jnp.where(lo2, k2, k2r).astype(BF16), jnp.where(lo2, k2r, k2).astype(BF16)]
    vd = [jnp.where(lo2, v2, v2r).astype(BF16), jnp.where(lo2, v2r, v2).astype(BF16)]
    qi = lax.broadcasted_iota(jnp.int32, (BLOCK, 2 * BLOCK), 0)
    kj = lax.broadcasted_iota(jnp.int32, (BLOCK, 2 * BLOCK), 1)
    valid = (kj > qi) & (kj <= qi + BLOCK) & ((kj >= BLOCK) | (i > 0))
    for p in range(N_PAIRS):
        h = p // (N_PAIRS // SWA_KV_HEADS)
        qp = _rope(q_ref[:, p * LANES:(p + 1) * LANES].astype(F32), cos_c, sin_c) * SCALE
        outs = []
        for e in range(2):
            qe = jnp.where(lo if e == 0 else jnp.logical_not(lo), qp, 0.0).astype(BF16)
            s = lax.dot_general(qe, kd[h], (((1,), (1,)), ((), ())), preferred_element_type=F32)
            s = jnp.where(valid, s, NEG_INF)
            sink = sink_ref[2 * p + e]
            m = jnp.maximum(jnp.max(s, axis=-1, keepdims=True), sink)
            pr = jnp.exp(s - m)
            den = jnp.sum(pr, axis=-1, keepdims=True) + jnp.exp(sink - m)
            o = jnp.dot(pr.astype(BF16), vd[h], preferred_element_type=F32)
            outs.append(o / den)
        o_ref[:, p * LANES:(p + 1) * LANES] = jnp.where(lo, outs[0], outs[1]).astype(BF16)


def _swa(z, cos_t, sin_t, sink, b, s):
    t = z.shape[0]
    nb = s // BLOCK
    cur = lambda bi, i: (bi * nb + i, 0)
    prev = lambda bi, i: (bi * nb + jnp.maximum(i - 1, 0), 0)
    zcur = lambda c: (lambda bi, i: (bi * nb + i, c))
    zprev = lambda c: (lambda bi, i: (bi * nb + jnp.maximum(i - 1, 0), c))
    return pl.pallas_call(
        _swa_kernel,
        grid=(b, nb),
        in_specs=[
            pl.BlockSpec(memory_space=pltpu.SMEM),
            pl.BlockSpec((BLOCK, MIX_W), zcur(Z_SQ // MIX_W)),
            pl.BlockSpec((BLOCK, LANES), zprev(Z_SK // LANES)),
            pl.BlockSpec((BLOCK, LANES), zcur(Z_SK // LANES)),
            pl.BlockSpec((BLOCK, LANES), zprev(Z_SV // LANES)),
            pl.BlockSpec((BLOCK, LANES), zcur(Z_SV // LANES)),
            pl.BlockSpec((BLOCK, LANES), cur),
            pl.BlockSpec((BLOCK, LANES), cur),
            pl.BlockSpec((BLOCK, LANES), prev),
            pl.BlockSpec((BLOCK, LANES), prev),
        ],
        out_specs=pl.BlockSpec((BLOCK, MIX_W), cur),
        out_shape=jax.ShapeDtypeStruct((t, MIX_W), BF16),
        compiler_params=_params("arbitrary", "arbitrary"),
        name="swa",
    )(sink, z, z, z, z, z, cos_t, sin_t, cos_t, sin_t)


def _fox_kernel(q_ref, k_ref, v_ref, ck_ref, o_ref):
    i = pl.program_id(2)
    lo = _lane_lo((BLOCK, LANES))
    q = q_ref[...].astype(F32) * SCALE
    q0 = jnp.where(lo, q, 0.0).astype(BF16)
    q1 = jnp.where(lo, 0.0, q).astype(BF16)
    qi = lax.broadcasted_iota(jnp.int32, (BLOCK, BLOCK), 0)
    kj = lax.broadcasted_iota(jnp.int32, (BLOCK, BLOCK), 1)
    diag = kj <= qi
    nt = (((1,), (1,)), ((), ()))

    def step(j, carry, masked):
        m0, l0, m1, l1, acc = carry
        start = pl.multiple_of(j * BLOCK, BLOCK)
        k2 = k_ref[pl.ds(start, BLOCK), :]
        v2 = v_ref[pl.ds(start, BLOCK), :]
        ck = ck_ref[0, 0, j]
        s0 = lax.dot_general(q0, k2, nt, preferred_element_type=F32) - ck[0:1, :]
        s1 = lax.dot_general(q1, k2, nt, preferred_element_type=F32) - ck[1:2, :]
        if masked:
            s0 = jnp.where(diag, s0, NEG_INF)
            s1 = jnp.where(diag, s1, NEG_INF)
        m0n = jnp.maximum(m0, jnp.max(s0, axis=-1, keepdims=True))
        m1n = jnp.maximum(m1, jnp.max(s1, axis=-1, keepdims=True))
        a0 = jnp.exp(m0 - m0n)
        a1 = jnp.exp(m1 - m1n)
        p0 = jnp.exp(s0 - m0n)
        p1 = jnp.exp(s1 - m1n)
        l0 = a0 * l0 + jnp.sum(p0, axis=-1, keepdims=True)
        l1 = a1 * l1 + jnp.sum(p1, axis=-1, keepdims=True)
        pv0 = jnp.dot(p0.astype(BF16), v2, preferred_element_type=F32)
        pv1 = jnp.dot(p1.astype(BF16), v2, preferred_element_type=F32)
        acc = jnp.where(lo, a0, a1) * acc + jnp.where(lo, pv0, pv1)
        return m0n, l0, m1n, l1, acc

    col = lambda v: jnp.full((BLOCK, 1), v, F32)
    init = (col(NEG_INF), col(0.0), col(NEG_INF), col(0.0), jnp.zeros((BLOCK, LANES), F32))
    carry = lax.fori_loop(0, i, lambda j, c: step(j, c, False), init)
    _, l0, _, l1, acc = step(i, carry, True)
    o_ref[...] = (acc / jnp.where(lo, l0, l1)).astype(BF16)


def _fox(z, ck, b, s):
    t = z.shape[0]
    nb = s // BLOCK
    return pl.pallas_call(
        _fox_kernel,
        grid=(b, N_PAIRS, nb),
        in_specs=[
            pl.BlockSpec((BLOCK, LANES), lambda bi, p, i: (bi * nb + i, Z_FQ // LANES + p)),
            pl.BlockSpec((s, LANES), lambda bi, p, i: (bi, Z_FK // LANES + p)),
            pl.BlockSpec((s, LANES), lambda bi, p, i: (bi, Z_FV // LANES + p)),
            pl.BlockSpec((1, 1, nb, 2, LANES), lambda bi, p, i: (bi, p, 0, 0, 0)),
        ],
        out_specs=pl.BlockSpec((BLOCK, LANES), lambda bi, p, i: (bi * nb + i, p)),
        out_shape=jax.ShapeDtypeStruct((t, MIX_W), BF16),
        compiler_params=_params("arbitrary", "arbitrary", "arbitrary"),
        name="fox",
    )(z, z, z, ck)


def _conv_kernel(ca_ref, cg_ref, cap_ref, cgp_ref, w_ref, b_ref, lng_ref, lnb_ref, o_ref, y_scr):
    i = pl.program_id(1)
    ts = ca_ref.shape[0]
    yp = cap_ref[...].astype(F32) * _sigmoid(cgp_ref[...].astype(F32))
    y_scr[0:CONV_HALO, :] = jnp.where(i > 0, yp, 0.0)
    y_scr[CONV_HALO:, :] = ca_ref[...].astype(F32) * _sigmoid(cg_ref[...].astype(F32))
    first = CONV_HALO - (CONV_TAPS - 1)
    for r in range(ts // CONV_RB):
        acc = jnp.zeros((CONV_RB, MIX_W), F32)
        for k in range(CONV_TAPS):
            off = r * CONV_RB + first + k
            acc = acc + w_ref[k:k + 1, :] * y_scr[off:off + CONV_RB, :]
        y = acc + b_ref[...]
        mu = jnp.mean(y, axis=-1, keepdims=True)
        yc = y - mu
        yn = yc * lax.rsqrt(jnp.mean(yc * yc, axis=-1, keepdims=True) + EPS)
        yn = yn * lng_ref[...] + lnb_ref[...]
        o_ref[r * CONV_RB:(r + 1) * CONV_RB, :] = (yn * _sigmoid(yn)).astype(BF16)


def _conv(z, w, bias, lng, lnb, b, s):
    t = z.shape[0]
    ts = min(CONV_T, s)
    nt = s // ts
    hb = ts // CONV_HALO
    cur = lambda c: (lambda bi, i: (bi * nt + i, c))
    prev = lambda c: (lambda bi, i: (jnp.maximum((bi * nt + i) * hb - 1, 0), c))
    return pl.pallas_call(
        _conv_kernel,
        grid=(b, nt),
        in_specs=[
            pl.BlockSpec((ts, MIX_W), cur(Z_CA // MIX_W)),
            pl.BlockSpec((ts, MIX_W), cur(Z_CG // MIX_W)),
            pl.BlockSpec((CONV_HALO, MIX_W), prev(Z_CA // MIX_W)),
            pl.BlockSpec((CONV_HALO, MIX_W), prev(Z_CG // MIX_W)),
            pl.BlockSpec((CONV_HALO, MIX_W), lambda bi, i: (0, 0)),
            pl.BlockSpec((1, MIX_W), lambda bi, i: (0, 0)),
            pl.BlockSpec((1, MIX_W), lambda bi, i: (0, 0)),
            pl.BlockSpec((1, MIX_W), lambda bi, i: (0, 0)),
        ],
        out_specs=pl.BlockSpec((ts, MIX_W), lambda bi, i: (bi * nt + i, 0)),
        out_shape=jax.ShapeDtypeStruct((t, MIX_W), BF16),
        scratch_shapes=[pltpu.VMEM((CONV_HALO + ts, MIX_W), F32)],
        compiler_params=_params("arbitrary", "arbitrary"),
        name="conv",
    )(z, z, z, z, w, bias, lng, lnb)


def _merge_kernel(zg_ref, oa_ref, ob_ref, oc_ref, od_ref, x_ref, p_ref, wo_ref, xo_ref):
    merged = None
    for n, o_ref in enumerate((oa_ref, ob_ref, oc_ref, od_ref)):
        proj = jnp.dot(o_ref[...], p_ref[n], preferred_element_type=F32)
        gate = _sigmoid(zg_ref[:, n * D_MODEL:(n + 1) * D_MODEL].astype(F32))
        merged = gate * proj if merged is None else merged + gate * proj
    out = jnp.dot(merged.astype(BF16), wo_ref[...], preferred_element_type=F32)
    xo_ref[...] = x_ref[...] + out


def _merge(z, oa, ob, oc, od, x, wp, wo):
    t = x.shape[0]
    tm = min(MERGE_TM, t)
    br = pl.BlockSpec((tm, MIX_W), lambda i: (i, 0))
    return pl.pallas_call(
        _merge_kernel,
        grid=(t // tm,),
        in_specs=[
            pl.BlockSpec((tm, N_BRANCH * D_MODEL), lambda i: (i, Z_GATE)),
            br, br, br, br,
            pl.BlockSpec((tm, D_MODEL), lambda i: (i, 0)),
            pl.BlockSpec((N_BRANCH, MIX_W, D_MODEL), lambda i: (0, 0, 0)),
            pl.BlockSpec((D_MODEL, D_MODEL), lambda i: (0, 0)),
        ],
        out_specs=pl.BlockSpec((tm, D_MODEL), lambda i: (i, 0)),
        out_shape=jax.ShapeDtypeStruct((t, D_MODEL), F32),
        compiler_params=_params("arbitrary"),
        name="merge",
    )(z, oa, ob, oc, od, x, wp, wo)


def _ffn_kernel(x_ref, g_ref, wg_ref, wu_ref, wd_ref, o_ref, h_scr, acc_scr):
    f = pl.program_id(1)

    @pl.when(f == 0)
    def _():
        h_scr[...] = _rms(x_ref[...], g_ref[...]).astype(BF16)
        acc_scr[...] = jnp.zeros_like(acc_scr)

    h = h_scr[...]
    a = jnp.dot(h, wg_ref[...], preferred_element_type=F32)
    u = jnp.dot(h, wu_ref[...], preferred_element_type=F32)
    act = (a * _sigmoid(a) * u).astype(BF16)
    acc_scr[...] += jnp.dot(act, wd_ref[...], preferred_element_type=F32)

    @pl.when(f == pl.num_programs(1) - 1)
    def _():
        o_ref[...] = x_ref[...] + acc_scr[...]


def _ffn(x, g, wg, wu, wd):
    t = x.shape[0]
    tm = min(FFN_TM, t)
    return pl.pallas_call(
        _ffn_kernel,
        grid=(t // tm, D_FF // FFN_TF),
        in_specs=[
            pl.BlockSpec((tm, D_MODEL), lambda i, f: (i, 0)),
            pl.BlockSpec((1, D_MODEL), lambda i, f: (0, 0)),
            pl.BlockSpec((D_MODEL, FFN_TF), lambda i, f: (0, f)),
            pl.BlockSpec((D_MODEL, FFN_TF), lambda i, f: (0, f)),
            pl.BlockSpec((FFN_TF, D_MODEL), lambda i, f: (f, 0)),
        ],
        out_specs=pl.BlockSpec((tm, D_MODEL), lambda i, f: (i, 0)),
        out_shape=jax.ShapeDtypeStruct((t, D_MODEL), F32),
        scratch_shapes=[pltpu.VMEM((tm, D_MODEL), BF16), pltpu.VMEM((tm, D_MODEL), F32)],
        compiler_params=_params("arbitrary", "arbitrary"),
        name="ffn_dense",
    )(x, g, wg, wu, wd)


def _router_kernel(x_ref, g_ref, rw_ref, h_ref, idx_ref, wts_ref):
    h = _rms(x_ref[...], g_ref[...])
    h_ref[...] = h
    logits = jnp.dot(h, rw_ref[...], preferred_element_type=F32, precision=HIGHEST)
    lane = lax.broadcasted_iota(jnp.int32, logits.shape, 1).astype(F32)
    lg = jnp.where(lane < N_EXPERTS, logits, -jnp.inf)
    m1 = jnp.max(lg, axis=-1, keepdims=True)
    i1 = jnp.min(jnp.where(lg == m1, lane, float(LANES)), axis=-1, keepdims=True)
    lg2 = jnp.where(lane == i1, -jnp.inf, lg)
    m2 = jnp.max(lg2, axis=-1, keepdims=True)
    i2 = jnp.min(jnp.where(lg2 == m2, lane, float(LANES)), axis=-1, keepdims=True)
    e = jnp.exp(m2 - m1)
    w1 = 1.0 / (1.0 + e)
    w2 = e / (1.0 + e)
    idx_ref[...] = jnp.where(lane == 0, i1, jnp.where(lane == 1, i2, 0.0)).astype(jnp.int32)
    wts_ref[...] = jnp.where(lane == 0, w1, jnp.where(lane == 1, w2, 0.0))


def _router(x, g, rw):
    t = x.shape[0]
    tm = min(ROUTE_TM, t)
    row = lambda i: (i, 0)
    return pl.pallas_call(
        _router_kernel,
        grid=(t // tm,),
        in_specs=[pl.BlockSpec((tm, D_MODEL), row),
                  pl.BlockSpec((1, D_MODEL), lambda i: (0, 0)),
                  pl.BlockSpec((D_MODEL, LANES), lambda i: (0, 0))],
        out_specs=[pl.BlockSpec((tm, D_MODEL), row),
                   pl.BlockSpec((tm, LANES), row),
                   pl.BlockSpec((tm, LANES), row)],
        out_shape=[jax.ShapeDtypeStruct((t, D_MODEL), F32),
                   jax.ShapeDtypeStruct((t, LANES), jnp.int32),
                   jax.ShapeDtypeStruct((t, LANES), F32)],
        compiler_params=_params("arbitrary"),
        name="router",
    )(x, g, rw)


def _gather_kernel(idx_ref, src_ref, dst_ref, sem):
    base = pl.program_id(0) * GATHER_CHUNK

    def row_copy(r):
        src_row = idx_ref[lax.shift_right_logical(r, 7), lax.bitwise_and(r, LANES - 1)]
        return pltpu.make_async_copy(src_ref.at[pl.ds(src_row, 1)],
                                     dst_ref.at[pl.ds(base + r, 1)], sem)

    def issue(r, c):
        row_copy(r).start()
        return c

    def drain(r, c):
        row_copy(r).wait()
        return c

    lax.fori_loop(0, GATHER_CHUNK, issue, 0)
    lax.fori_loop(0, GATHER_CHUNK, drain, 0)


def _gather_rows(src, idx):
    m = idx.shape[0]
    idx2 = idx.reshape(m // LANES, LANES)
    return pl.pallas_call(
        _gather_kernel,
        grid=(m // GATHER_CHUNK,),
        in_specs=[pl.BlockSpec((GATHER_CHUNK // LANES, LANES), lambda i: (i, 0),
                               memory_space=pltpu.SMEM),
                  pl.BlockSpec(memory_space=pl.ANY)],
        out_specs=pl.BlockSpec(memory_space=pl.ANY),
        out_shape=jax.ShapeDtypeStruct((m, D_MODEL), F32),
        scratch_shapes=[pltpu.SemaphoreType.DMA(())],
        compiler_params=_params("arbitrary"),
        name="gather_rows",
    )(idx2, src)


def _moe_kernel(te_ref, tv_ref, xs_ref, rw_ref, wg_ref, wu_ref, wd_ref, o_ref, h_scr, acc_scr):
    j = pl.program_id(0)
    f = pl.program_id(1)
    live = tv_ref[j] > 0

    @pl.when(f == 0)
    def _():
        h_scr[...] = xs_ref[...].astype(BF16)
        acc_scr[...] = jnp.zeros_like(acc_scr)

    @pl.when(live)
    def _():
        h = h_scr[...]
        a = jnp.dot(h, wg_ref[0], preferred_element_type=F32)
        u = jnp.dot(h, wu_ref[0], preferred_element_type=F32)
        act = (a * _sigmoid(a) * u).astype(BF16)
        acc_scr[...] += jnp.dot(act, wd_ref[0], preferred_element_type=F32)

    @pl.when(f == pl.num_programs(1) - 1)
    def _():
        o_ref[...] = acc_scr[...] * rw_ref[...]


def _moe_ffn(tile_expert, tile_valid, xs, roww, wg, wu, wd):
    r = xs.shape[0]
    nf = D_FF // FFN_TF
    fsel = lambda j, f, te, tv: jnp.where(tv[j] > 0, f, nf - 1)
    grid_spec = pltpu.PrefetchScalarGridSpec(
        num_scalar_prefetch=2,
        grid=(r // MOE_TR, nf),
        in_specs=[
            pl.BlockSpec((MOE_TR, D_MODEL), lambda j, f, te, tv: (j, 0)),
            pl.BlockSpec((MOE_TR, 1), lambda j, f, te, tv: (j, 0)),
            pl.BlockSpec((1, D_MODEL, FFN_TF), lambda j, f, te, tv: (te[j], 0, fsel(j, f, te, tv))),
            pl.BlockSpec((1, D_MODEL, FFN_TF), lambda j, f, te, tv: (te[j], 0, fsel(j, f, te, tv))),
            pl.BlockSpec((1, FFN_TF, D_MODEL), lambda j, f, te, tv: (te[j], fsel(j, f, te, tv), 0)),
        ],
        out_specs=pl.BlockSpec((MOE_TR, D_MODEL), lambda j, f, te, tv: (j, 0)),
        scratch_shapes=[pltpu.VMEM((MOE_TR, D_MODEL), BF16), pltpu.VMEM((MOE_TR, D_MODEL), F32)],
    )
    return pl.pallas_call(
        _moe_kernel,
        grid_spec=grid_spec,
        out_shape=jax.ShapeDtypeStruct((r, D_MODEL), F32),
        compiler_params=_params("arbitrary", "arbitrary"),
        name="moe_ffn",
    )(tile_expert, tile_valid, xs, roww, wg, wu, wd)


def _combine_kernel(x_ref, y0_ref, y1_ref, g_ref, o_ref):
    o_ref[...] = _rms(x_ref[...] + y0_ref[...] + y1_ref[...], g_ref[...])


def _combine(x, y2, g):
    t = x.shape[0]
    tm = min(COMBINE_TM, t)
    nt = t // tm
    return pl.pallas_call(
        _combine_kernel,
        grid=(nt,),
        in_specs=[pl.BlockSpec((tm, D_MODEL), lambda i: (i, 0)),
                  pl.BlockSpec((tm, D_MODEL), lambda i: (i, 0)),
                  pl.BlockSpec((tm, D_MODEL), lambda i: (i + nt, 0)),
                  pl.BlockSpec((1, D_MODEL), lambda i: (0, 0))],
        out_specs=pl.BlockSpec((tm, D_MODEL), lambda i: (i, 0)),
        out_shape=jax.ShapeDtypeStruct((t, D_MODEL), F32),
        compiler_params=_params("arbitrary"),
        name="combine_norm",
    )(x, y2, y2, g)


def _route_plan(idx, wts, t):
    e_flat = jnp.concatenate([idx[:, 0], idx[:, 1]])
    w_flat = jnp.concatenate([wts[:, 0], wts[:, 1]])
    tok = jnp.concatenate([jnp.arange(t, dtype=jnp.int32)] * 2)
    onehot = (e_flat[:, None] == jnp.arange(N_EXPERTS, dtype=jnp.int32)[None, :]).astype(jnp.int32)
    csum = jnp.cumsum(onehot, axis=0)
    rank = jnp.sum((csum - onehot) * onehot, axis=1)
    counts = csum[-1]
    padded = ((counts + MOE_TR - 1) // MOE_TR) * MOE_TR
    ends = jnp.cumsum(padded)
    starts = ends - padded
    pos = starts[e_flat] + rank
    n_rows = ((2 * t + N_EXPERTS * (MOE_TR - 1)) // GATHER_CHUNK + 1) * GATHER_CHUNK
    src_tok = jnp.zeros((n_rows,), jnp.int32).at[pos].set(tok)
    roww = jnp.zeros((n_rows,), F32).at[pos].set(w_flat)
    tile_start = jnp.arange(n_rows // MOE_TR, dtype=jnp.int32) * MOE_TR
    tile_valid = (tile_start < ends[-1]).astype(jnp.int32)
    tile_expert = jnp.minimum(jnp.searchsorted(ends, tile_start, side="right"),
                              N_EXPERTS - 1).astype(jnp.int32)
    last_expert = tile_expert[jnp.maximum(ends[-1] // MOE_TR - 1, 0)]
    tile_expert = jnp.where(tile_valid > 0, tile_expert, last_expert)
    return src_tok, roww.reshape(n_rows, 1), pos.astype(jnp.int32), tile_expert, tile_valid


def _pack_w_in(w):
    segs = {}
    off = 0
    for name, n in (("zu", 512), ("zv", 512), ("sq", 512), ("sk", 128), ("sv", 128), ("fq", 512),
                    ("fk", 512), ("fv", 512), ("ff", 8), ("ca", 512), ("cg", 512), ("gate", 4096)):
        segs[name] = w[:, off:off + n]
        off += n
    order = ("gate", "zu", "zv", "sq", "fq", "fk", "fv", "ca", "cg", "sk", "sv")
    packed = jnp.concatenate([segs[k] for k in order], axis=1).astype(BF16)
    wff = jnp.pad(segs["ff"], ((0, 0), (0, LANES - N_HEADS)))
    return packed, wff


def kernel(x, positions, norm_mix_g, w_in, gmlp_ln_g, gmlp_ln_b, gmlp_ws, gmlp_bs, swa_sink, fox_bf, conv_w, conv_b, conv_ln_g, conv_ln_b, w_branch, w_out, norm_ffn_g, ffn_w_gate, ffn_w_up, ffn_w_down, router_w, exp_w_gate, exp_w_up, exp_w_down, norm_final_g):
    b, s, d = x.shape
    t = b * s
    depth = w_in.shape[0]
    x2 = x.reshape(t, d)
    cos_t, sin_t = _rope_tables(positions.reshape(t, 1))
    row = lambda v: v.reshape(1, -1)
    out = None
    for l in range(depth):
        w_packed, wff = _pack_w_in(w_in[l])
        z, zff = _proj(x2, row(norm_mix_g[l]), w_packed, wff)
        bf = jnp.pad(fox_bf[l], (0, LANES - N_HEADS)).reshape(1, LANES)
        ck = _cum(zff, bf, b, s)
        gbias = jnp.repeat(gmlp_bs[l].T, HEAD_DIM, axis=1)
        oa = _gmlp(z, row(gmlp_ln_g[l]), row(gmlp_ln_b[l]), gmlp_ws[l], gbias)
        ob = _swa(z, cos_t, sin_t, swa_sink[l], b, s)
        oc = _fox(z, ck, b, s)
        cw = jnp.pad(conv_w[l], ((0, CONV_HALO - CONV_TAPS), (0, 0)))
        od = _conv(z, cw, row(conv_b[l]), row(conv_ln_g[l]), row(conv_ln_b[l]), b, s)
        x2 = _merge(z, oa, ob, oc, od, x2, w_branch[l].astype(BF16), w_out[l].astype(BF16))
        j = l // 2
        if l % 2 == 0:
            x2 = _ffn(x2, row(norm_ffn_g[l]), ffn_w_gate[j].astype(BF16),
                      ffn_w_up[j].astype(BF16), ffn_w_down[j].astype(BF16))
            if l == depth - 1:
                zero = jnp.zeros((2 * t, d), F32)
                out = _combine(x2, zero, row(norm_final_g))
        else:
            rw = jnp.pad(router_w[j], ((0, 0), (0, LANES - N_EXPERTS)))
            h, idx, wts = _router(x2, row(norm_ffn_g[l]), rw)
            src_tok, roww, pos, tile_expert, tile_valid = _route_plan(idx, wts, t)
            xs = _gather_rows(h, src_tok)
            ys = _moe_ffn(tile_expert, tile_valid, xs, roww, exp_w_gate[j].astype(BF16),
                          exp_w_up[j].astype(BF16), exp_w_down[j].astype(BF16))
            y2 = _gather_rows(ys, pos)
            if l == depth - 1:
                out = _combine(x2, y2, row(norm_final_g))
            else:
                x2 = x2 + y2[:t] + y2[t:]
    return out.reshape(b, s, d)
```
